```python
import jax, jax.numpy as jnp
from jax import lax
import numpy as np

D_MODEL = 1024
BATCH = 4
SEQ = 4096
DEPTH = 1

CHUNK = 64
N_META = 16
Q_BLOCK = 128
EPS = 1e-6
NEG_INF = -1e30
ROPE_BASE = 10000.0

FOX_HEADS = 8
FOX_HEAD_DIM = D_MODEL // (2 * FOX_HEADS)
FOX_WIDTH = FOX_HEADS * FOX_HEAD_DIM
MLA_HEADS = 8
MLA_NOPE_DIM = 64
MLA_ROPE_DIM = 32
MLA_V_DIM = D_MODEL // (2 * MLA_HEADS)
MLA_Q_RANK = D_MODEL // 4
MLA_KV_RANK = D_MODEL // 8
MLA_QK_DIM = MLA_NOPE_DIM + MLA_ROPE_DIM
MLA_WIDTH = MLA_HEADS * MLA_V_DIM
D_MIX = FOX_WIDTH + MLA_WIDTH

IN_SPLITS = (FOX_WIDTH, FOX_WIDTH, FOX_WIDTH, FOX_HEADS, FOX_WIDTH,
             MLA_Q_RANK, MLA_KV_RANK, MLA_ROPE_DIM, MLA_WIDTH)
D_IN = sum(IN_SPLITS)

kernel_name = "hybrid_fox_mla_parallel_heads"


def _rmsnorm(x, g):
    xf = x.astype(jnp.float32)
    r = lax.rsqrt(jnp.mean(xf * xf, axis=-1, keepdims=True) + EPS)
    return (xf * r * g.astype(jnp.float32)).astype(x.dtype)


def _rope(x, cos, sin):
    half = x.shape[-1] // 2
    x1, x2 = x[..., :half], x[..., half:]
    c, s = cos[None, :, None, :], sin[None, :, None, :]
    return jnp.concatenate([x1 * c - x2 * s, x1 * s + x2 * c], axis=-1)


def _chunk_id(p):
    return jnp.where(p < N_META, 0, 1 + (p - N_META) // CHUNK)


def _chunk_end(p):
    if p < N_META:
        return N_META
    return N_META + ((p - N_META) // CHUNK + 1) * CHUNK


def _block_sweep(q, k, v, scale, key_end, mask_and_bias):
    L = q.shape[1]
    outs = []
    for q0 in range(0, L, Q_BLOCK):
        kend = key_end(q0)
        s = jnp.einsum('bqhd,bkhd->bhqk', q[:, q0:q0 + Q_BLOCK], k[:, :kend]).astype(jnp.float32) * scale
        s = mask_and_bias(s, q0, kend)
        p = jax.nn.softmax(s, axis=-1).astype(v.dtype)
        outs.append(jnp.einsum('bhqk,bkhd->bqhd', p, v[:, :kend]))
    return jnp.concatenate(outs, axis=1)


def _layer(x, norm_pre, norm_post, w_in, b_f, q_norm, w_uq, kv_norm, w_ukv, w_out, cos, sin):
    B, L, _ = x.shape
    h = _rmsnorm(x, norm_pre)
    proj = h @ w_in
    (fq, fk, fv, f_logit, f_gate, c_q, c_kv, k_r, m_gate) = jnp.split(
        proj, np.cumsum(IN_SPLITS)[:-1].tolist(), axis=-1)

    fq = fq.reshape(B, L, FOX_HEADS, FOX_HEAD_DIM)
    fk = fk.reshape(B, L, FOX_HEADS, FOX_HEAD_DIM)
    fv = fv.reshape(B, L, FOX_HEADS, FOX_HEAD_DIM)
    log_f = jax.nn.log_sigmoid((f_logit + b_f).astype(jnp.float32))
    cum = jnp.cumsum(log_f, axis=1).transpose(0, 2, 1)

    def fox_mask_bias(s, q0, kend):
        qpos = q0 + jnp.arange(Q_BLOCK)
        kpos = jnp.arange(kend)
        decay = cum[:, :, q0:q0 + Q_BLOCK, None] - cum[:, :, None, :kend]
        return jnp.where(kpos[None, :] <= qpos[:, None], s + decay, NEG_INF)

    fox = _block_sweep(fq, fk, fv, FOX_HEAD_DIM ** -0.5,
                       lambda q0: q0 + Q_BLOCK, fox_mask_bias)
    fox = fox.reshape(B, L, FOX_WIDTH) * jax.nn.silu(f_gate)

    q = (_rmsnorm(c_q, q_norm) @ w_uq).reshape(B, L, MLA_HEADS, MLA_QK_DIM)
    q = jnp.concatenate([q[..., :MLA_NOPE_DIM], _rope(q[..., MLA_NOPE_DIM:], cos, sin)], axis=-1)
    kv = (_rmsnorm(c_kv, kv_norm) @ w_ukv).reshape(B, L, MLA_HEADS, MLA_NOPE_DIM + MLA_V_DIM)
    k_nope, mv = kv[..., :MLA_NOPE_DIM], kv[..., MLA_NOPE_DIM:]
    k_rope = _rope(k_r[:, :, None, :], cos, sin)
    k = jnp.concatenate([k_nope, jnp.broadcast_to(k_rope, (B, L, MLA_HEADS, MLA_ROPE_DIM))], axis=-1)

    def mla_mask(s, q0, kend):
        cq = _chunk_id(q0 + jnp.arange(Q_BLOCK))
        ck = _chunk_id(jnp.arange(kend))
        return jnp.where(ck[None, :] <= cq[:, None], s, NEG_INF)

    mla = _block_sweep(q, k, mv, MLA_QK_DIM ** -0.5,
                       lambda q0: min(L, _chunk_end(q0 + Q_BLOCK - 1)), mla_mask)
    mla = mla.reshape(B, L, MLA_WIDTH) * jax.nn.silu(m_gate)

    y = jnp.concatenate([fox, mla], axis=-1) @ w_out
    return x + _rmsnorm(y, norm_post)


def setup_inputs(seed: int = 0) -> dict:
    key = jax.random.key(seed)
    ks = jax.random.split(key, 14)
    n = jax.random.normal
    f32 = jnp.float32
    return {
        "x": n(ks[0], (BATCH, SEQ, D_MODEL), f32),
        "meta": n(ks[1], (N_META, D_MODEL), f32),
        "norm_pre": 1.0 + 0.05 * n(ks[2], (DEPTH, D_MODEL), f32),
        "norm_post": 1.0 + 0.05 * n(ks[3], (DEPTH, D_MODEL), f32),
        "w_in": n(ks[4], (DEPTH, D_MODEL, D_IN), f32) * D_MODEL ** -0.5,
        "b_f": 2.0 + 0.5 * n(ks[5], (DEPTH, FOX_HEADS), f32),
        "q_norm": 1.0 + 0.05 * n(ks[6], (DEPTH, MLA_Q_RANK), f32),
        "w_uq": n(ks[7], (DEPTH, MLA_Q_RANK, MLA_HEADS * MLA_QK_DIM), f32) * MLA_Q_RANK ** -0.5,
        "kv_norm": 1.0 + 0.05 * n(ks[8], (DEPTH, MLA_KV_RANK), f32),
        "w_ukv": n(ks[9], (DEPTH, MLA_KV_RANK, MLA_HEADS * (MLA_NOPE_DIM + MLA_V_DIM)), f32) * MLA_KV_RANK ** -0.5,
        "w_out": n(ks[10], (DEPTH, D_MIX, D_MODEL), f32) * D_MIX ** -0.5,
    }


def reference(x, meta, norm_pre, norm_post, w_in, b_f, q_norm, w_uq, kv_norm, w_ukv, w_out):
    B, S, D = x.shape
    L = S + N_META
    Lp = ((L + Q_BLOCK - 1) // Q_BLOCK) * Q_BLOCK
    h = jnp.concatenate([
        jnp.broadcast_to(meta.astype(x.dtype)[None], (B, N_META, D)),
        x,
        jnp.zeros((B, Lp - L, D), x.dtype)], axis=1)

    pos = jnp.arange(Lp, dtype=jnp.float32)
    inv_freq = ROPE_BASE ** (-jnp.arange(0, MLA_ROPE_DIM, 2, dtype=jnp.float32) / MLA_ROPE_DIM)
    ang = pos[:, None] * inv_freq[None, :]
    cos, sin = jnp.cos(ang).astype(x.dtype), jnp.sin(ang).astype(x.dtype)

    for l in range(DEPTH):
        h = _layer(h, norm_pre[l], norm_post[l], w_in[l], b_f[l], q_norm[l], w_uq[l],
                   kv_norm[l], w_ukv[l], w_out[l], cos, sin)
    return h[:, N_META:N_META + S]
```

```python
import functools

import jax
import jax.numpy as jnp
import numpy as np
from jax import lax
from jax.experimental import pallas as pl
from jax.experimental.pallas import tpu as pltpu

F32 = jnp.float32
BF16 = jnp.bfloat16

LANES = 128
EPS = 1e-6
NEG_INF = -1e30
ROPE_BASE = 10000.0
CHUNK = 64
CHUNK_SHIFT = 6
N_META = 16
META_TILE = 128

HEADS = 8
HEAD_DIM = 64
ROPE_DIM = 32
HALF_ROPE = ROPE_DIM // 2
Q_RANK = 256
KV_RANK = 128
FOX_SCALE = HEAD_DIM ** -0.5
MLA_SCALE = (HEAD_DIM + ROPE_DIM) ** -0.5

FOX_BIAS_LANE = HEAD_DIM
MLA_PAD_LANE = HEAD_DIM + ROPE_DIM

PROJ_ROWS = 512
ATT_T = 256
VMEM_LIMIT = 56 * 1024 * 1024


def _rms(x, g):
    return x * lax.rsqrt(jnp.mean(x * x, axis=-1, keepdims=True) + EPS) * g


def _split3(x):
    hi = x.astype(BF16).astype(F32)
    r = x - hi
    mid = r.astype(BF16).astype(F32)
    lo = (r - mid).astype(BF16).astype(F32)
    return hi, mid, lo


def _rope(v, c, s1, s2):
    return v * c + pltpu.roll(v, LANES - HALF_ROPE, 1) * s1 + pltpu.roll(v, HALF_ROPE, 1) * s2


def _proj_kernel(x_ref, gpre_ref, wmain_ref, bf_ref, tri_ref, emat_ref, qn_ref, wuq_ref,
                 kvn_ref, wukv_ref, rc_ref, rs1_ref, rs2_ref,
                 q_ref, k_ref, vt_ref, gate_ref, carry_ref, *, n_real, kv_tile):
    tl = x_ref.shape[1]
    fw = HEADS * HEAD_DIM

    @pl.when(pl.program_id(1) == 0)
    def _():
        carry_ref[...] = jnp.zeros_like(carry_ref)

    x = x_ref[0]
    h = _rms(x, gpre_ref[...]).astype(BF16)
    proj = jnp.dot(h, wmain_ref[...], preferred_element_type=F32)
    o_fq, o_fk, o_fv, o_gate, o_cq, o_ckv, o_small = 0, fw, 2 * fw, 3 * fw, 5 * fw, 5 * fw + Q_RANK, 5 * fw + Q_RANK + KV_RANK

    lane = lax.broadcasted_iota(jnp.int32, (tl, LANES), 1)
    rowi = lax.broadcasted_iota(jnp.int32, (tl, LANES), 0)

    small = proj[:, o_small:o_small + LANES]
    z = small + bf_ref[...]
    lf = jnp.minimum(z, 0.0) - jnp.log1p(jnp.exp(-jnp.abs(z)))
    hi, mid, lo = _split3(lf)
    lf_split = jnp.where(lane < 8, hi, jnp.where(lane < 16, mid, jnp.where(lane < 24, lo, 0.0)))
    c3 = jnp.dot(tri_ref[...], lf_split.astype(BF16), preferred_element_type=F32)
    cum = c3 + pltpu.roll(c3, LANES - 8, 1) + pltpu.roll(c3, LANES - 16, 1)
    cum = cum + carry_ref[...]
    carry_ref[...] = cum[tl - 1:tl, :]
    if n_real is None:
        nck = -cum
    else:
        nck = jnp.where(rowi < n_real, cum[n_real - 1:n_real, :] - cum, NEG_INF)
    m8 = lane < 8
    nh, nm, nl = _split3(nck)
    ck_split = (jnp.where(m8, nh, 0.0) + pltpu.roll(jnp.where(m8, nm, 0.0), 8, 1)
                + pltpu.roll(jnp.where(m8, nl, 0.0), 16, 1))
    extras = jnp.dot(ck_split.astype(BF16), emat_ref[...], preferred_element_type=F32)

    q_const = jnp.where((lane >= FOX_BIAS_LANE) & (lane < FOX_BIAS_LANE + 3), 1.0, 0.0)
    head_lane = lane < HEAD_DIM
    for hp in range(HEADS // 2):
        qv = proj[:, o_fq + hp * LANES:o_fq + (hp + 1) * LANES] * FOX_SCALE
        kv = proj[:, o_fk + hp * LANES:o_fk + (hp + 1) * LANES]
        for sub in range(2):
            g = 2 * hp + sub
            qs = qv if sub == 0 else pltpu.roll(qv, HEAD_DIM, 1)
            ks = kv if sub == 0 else pltpu.roll(kv, HEAD_DIM, 1)
            q_ref[0, :, g * LANES:(g + 1) * LANES] = jnp.where(head_lane, qs, q_const).astype(BF16)
            k_ref[0, :, g * LANES:(g + 1) * LANES] = jnp.where(
                head_lane, ks, extras[:, g * LANES:(g + 1) * LANES]).astype(BF16)

    rc, rs1, rs2 = rc_ref[...], rs1_ref[...], rs2_ref[...]
    cqn = _rms(proj[:, o_cq:o_cq + Q_RANK], qn_ref[...]).astype(BF16)
    qa = jnp.dot(cqn, wuq_ref[...], preferred_element_type=F32) * MLA_SCALE
    q_one = jnp.where(lane == MLA_PAD_LANE, 1.0, 0.0)
    for hh in range(HEADS):
        g = HEADS + hh
        qh = qa[:, hh * LANES:(hh + 1) * LANES]
        q_ref[0, :, g * LANES:(g + 1) * LANES] = (_rope(qh, rc, rs1, rs2) + q_one).astype(BF16)

    ckvn = _rms(proj[:, o_ckv:o_ckv + KV_RANK], kvn_ref[...]).astype(BF16)
    kvu = jnp.dot(ckvn, wukv_ref[...], preferred_element_type=F32)
    kr = _rope(small, rc, rs1, rs2)
    kr = jnp.where((lane >= HEAD_DIM) & (lane < HEAD_DIM + ROPE_DIM), kr, 0.0)
    if n_real is not None:
        kr = jnp.where((rowi >= n_real) & (lane == MLA_PAD_LANE), NEG_INF, kr)
    for hh in range(HEADS):
        g = HEADS + hh
        k_ref[0, :, g * LANES:(g + 1) * LANES] = (kvu[:, hh * LANES:(hh + 1) * LANES] + kr).astype(BF16)

    vcat = jnp.concatenate([proj[:, o_fv:o_fv + fw], kvu[:, HEADS * LANES:HEADS * LANES + fw]], axis=1)
    for s in range(tl // kv_tile):
        vt_ref[0, s] = vcat[s * kv_tile:(s + 1) * kv_tile, :].T.astype(BF16)

    gt = proj[:, o_gate:o_gate + 2 * fw]
    gate_ref[0] = (gt / (1.0 + jnp.exp(-gt))).astype(BF16)


def _run_proj(x3, consts, tabs, *, n_real, kv_tile, rows):
    b, s, d = x3.shape
    nt = s // rows
    gpre, wmain, bfrow, tri, emat, qn, wuq, kvn, wukv = consts
    rc, rs1, rs2 = tabs
    nh = 2 * HEADS
    const = lambda a: pl.BlockSpec(a.shape, lambda bi, i: (0,) * a.ndim)
    tab = pl.BlockSpec((rows, LANES), lambda bi, i: (i, 0))
    return pl.pallas_call(
        functools.partial(_proj_kernel, n_real=n_real, kv_tile=kv_tile),
        grid=(b, nt),
        in_specs=[pl.BlockSpec((1, rows, d), lambda bi, i: (bi, i, 0)),
                  const(gpre), const(wmain), const(bfrow), const(tri), const(emat), const(qn),
                  const(wuq), const(kvn), const(wukv), tab, tab, tab],
        out_specs=[pl.BlockSpec((1, rows, nh * LANES), lambda bi, i: (bi, i, 0)),
                   pl.BlockSpec((1, rows, nh * LANES), lambda bi, i: (bi, i, 0)),
                   pl.BlockSpec((1, rows // kv_tile, nh * HEAD_DIM, kv_tile), lambda bi, i: (bi, i, 0, 0)),
                   pl.BlockSpec((1, rows, nh * HEAD_DIM), lambda bi, i: (bi, i, 0))],
        out_shape=[jax.ShapeDtypeStruct((b, s, nh * LANES), BF16),
                   jax.ShapeDtypeStruct((b, s, nh * LANES), BF16),
                   jax.ShapeDtypeStruct((b, s // kv_tile, nh * HEAD_DIM, kv_tile), BF16),
                   jax.ShapeDtypeStruct((b, s, nh * HEAD_DIM), BF16)],
        scratch_shapes=[pltpu.VMEM((1, LANES), F32)],
        compiler_params=pltpu.CompilerParams(
            dimension_semantics=("arbitrary", "arbitrary"), vmem_limit_bytes=VMEM_LIMIT),
        name="proj",
    )(x3, gpre, wmain, bfrow, tri, emat, qn, wuq, kvn, wukv, rc, rs1, rs2)


def _dot_nt(a, b):
    return lax.dot_general(a, b, (((1,), (1,)), ((), ())), preferred_element_type=F32)


def _attn_kernel(q_ref, k_ref, vt_ref, km_ref, vmt_ref, gate_ref, o_ref, *, t):
    s_len = q_ref.shape[1]
    pair = pl.program_id(1)
    shift = jnp.where(pair >= HEADS // 2, CHUNK_SHIFT, 0)
    row = lax.broadcasted_iota(jnp.int32, (t, t), 0)
    col = lax.broadcasted_iota(jnp.int32, (t, t), 1)
    diag_ok = jnp.right_shift(row, shift) <= jnp.right_shift(col, shift)
    out_row = lax.broadcasted_iota(jnp.int32, (LANES, t), 0)

    def tile_update(state, qs, k0, vt, masked):
        new = []
        for hd in range(2):
            m, l, acc = state[3 * hd:3 * hd + 3]
            kt = k_ref[0, pl.ds(k0, t), hd * LANES:(hd + 1) * LANES]
            s = _dot_nt(kt, qs[hd])
            if masked:
                s = jnp.where(diag_ok, s, NEG_INF)
            m_new = jnp.maximum(m, jnp.max(s, axis=0, keepdims=True))
            alpha = jnp.exp(m - m_new)
            p = jnp.exp(s - m_new)
            l = alpha * l + jnp.sum(p, axis=0, keepdims=True)
            acc = alpha * acc + jnp.dot(vt, p.astype(BF16), preferred_element_type=F32)
            new += [m_new, l, acc]
        return tuple(new)

    def q_block(i, carry):
        q0 = pl.multiple_of(i * t, t)
        qs = [q_ref[0, pl.ds(q0, t), hd * LANES:(hd + 1) * LANES] for hd in range(2)]
        state = []
        for hd in range(2):
            s = _dot_nt(km_ref[:, hd * LANES:(hd + 1) * LANES], qs[hd])
            m = jnp.max(s, axis=0, keepdims=True)
            p = jnp.exp(s - m)
            l = jnp.sum(p, axis=0, keepdims=True)
            acc = jnp.dot(vmt_ref[0], p.astype(BF16), preferred_element_type=F32)
            state += [m, l, acc]
        state = tuple(state)

        def full_tile(j, st):
            return tile_update(st, qs, pl.multiple_of(j * t, t), vt_ref[0, j], False)

        state = lax.fori_loop(0, i, full_tile, state)
        state = tile_update(state, qs, q0, vt_ref[0, i], True)
        o0 = state[2] / state[1]
        o1 = state[5] / state[4]
        o = jnp.where(out_row < HEAD_DIM, o0, o1)
        gate = gate_ref[0, pl.ds(q0, t), :].astype(F32)
        o_ref[0, pl.ds(q0, t), :] = (o.T * gate).astype(BF16)
        return carry

    lax.fori_loop(0, s_len // t, q_block, 0)


def _run_attn(q, k, vt, km, vmt, gates, *, t):
    b, s, _ = q.shape
    npairs = HEADS
    return pl.pallas_call(
        functools.partial(_attn_kernel, t=t),
        grid=(b, npairs),
        in_specs=[pl.BlockSpec((1, s, 2 * LANES), lambda bi, p: (bi, 0, p)),
                  pl.BlockSpec((1, s, 2 * LANES), lambda bi, p: (bi, 0, p)),
                  pl.BlockSpec((1, s // t, LANES, t), lambda bi, p: (bi, 0, p, 0)),
                  pl.BlockSpec((META_TILE, 2 * LANES), lambda bi, p: (0, p)),
                  pl.BlockSpec((1, LANES, META_TILE), lambda bi, p: (p, 0, 0)),
                  pl.BlockSpec((1, s, LANES), lambda bi, p: (bi, 0, p))],
        out_specs=pl.BlockSpec((1, s, LANES), lambda bi, p: (bi, 0, p)),
        out_shape=jax.ShapeDtypeStruct((b, s, npairs * LANES), BF16),
        compiler_params=pltpu.CompilerParams(
            dimension_semantics=("arbitrary", "arbitrary"), vmem_limit_bytes=VMEM_LIMIT),
        name="attn",
    )(q, k, vt, km, vmt, gates)


def _out_kernel(x_ref, mix_ref, w_ref, g_ref, o_ref):
    y = jnp.dot(mix_ref[...], w_ref[...], preferred_element_type=F32)
    o_ref[...] = x_ref[...] + _rms(y, g_ref[...])


def _run_out(x2, mix2, w, g, *, rows):
    n, d = x2.shape
    return pl.pallas_call(
        _out_kernel,
        grid=(n // rows,),
        in_specs=[pl.BlockSpec((rows, d), lambda i: (i, 0)),
                  pl.BlockSpec((rows, mix2.shape[1]), lambda i: (i, 0)),
                  pl.BlockSpec(w.shape, lambda i: (0, 0)),
                  pl.BlockSpec(g.shape, lambda i: (0, 0))],
        out_specs=pl.BlockSpec((rows, d), lambda i: (i, 0)),
        out_shape=jax.ShapeDtypeStruct((n, d), F32),
        compiler_params=pltpu.CompilerParams(
            dimension_semantics=("arbitrary",), vmem_limit_bytes=VMEM_LIMIT),
        name="outproj",
    )(x2, mix2, w, g)


def _rope_tables(pos):
    inv_freq = ROPE_BASE ** (-jnp.arange(0, ROPE_DIM, 2, dtype=F32) / ROPE_DIM)
    ang = pos[:, None] * inv_freq[None, :]
    cos, sin = jnp.cos(ang), jnp.sin(ang)
    n = pos.shape[0]
    z = lambda w: jnp.zeros((n, w), F32)
    rc = jnp.concatenate([jnp.ones((n, HEAD_DIM), F32), cos, cos, z(LANES - HEAD_DIM - ROPE_DIM)], axis=1)
    rs1 = jnp.concatenate([z(HEAD_DIM), -sin, z(LANES - HEAD_DIM - HALF_ROPE)], axis=1)
    rs2 = jnp.concatenate([z(HEAD_DIM + HALF_ROPE), sin, z(LANES - HEAD_DIM - ROPE_DIM)], axis=1)
    return rc, rs1, rs2


def _layer_consts(norm_pre, w_in, b_f, q_norm, w_uq, kv_norm, w_ukv):
    d = w_in.shape[0]
    fw = HEADS * HEAD_DIM
    splits = np.cumsum([fw, fw, fw, HEADS, fw, Q_RANK, KV_RANK, ROPE_DIM, fw])
    w_fq, w_fk, w_fv, w_fl, w_fg, w_cq, w_ckv, w_kr, w_mg = jnp.split(w_in, splits[:-1].tolist(), axis=1)
    zc = lambda w: jnp.zeros((d, w), w_in.dtype)
    w_small = jnp.concatenate([w_fl, w_fl, w_fl, zc(HEAD_DIM - 3 * HEADS), w_kr,
                               zc(LANES - HEAD_DIM - ROPE_DIM)], axis=1)
    wmain = jnp.concatenate([w_fq, w_fk, w_fv, w_fg, w_mg, w_cq, w_ckv, w_small], axis=1).astype(BF16)
    bfrow = jnp.concatenate([b_f, b_f, b_f, jnp.zeros((LANES - 3 * HEADS,), F32)])[None, :]
    tri = jnp.tril(jnp.ones((PROJ_ROWS, PROJ_ROWS), BF16))
    e = np.zeros((LANES, HEADS * LANES), np.float32)
    for piece in range(3):
        for hh in range(HEADS):
            e[piece * HEADS + hh, hh * LANES + FOX_BIAS_LANE + piece] = 1.0
    emat = jnp.asarray(e, BF16)
    qk = HEAD_DIM + ROPE_DIM
    wuq = jnp.pad(w_uq.reshape(Q_RANK, HEADS, qk), ((0, 0), (0, 0), (0, LANES - qk)))
    wuq = wuq.reshape(Q_RANK, HEADS * LANES).astype(BF16)
    wkv = w_ukv.reshape(KV_RANK, HEADS, 2 * HEAD_DIM)
    wk = jnp.pad(wkv[:, :, :HEAD_DIM], ((0, 0), (0, 0), (0, LANES - HEAD_DIM))).reshape(KV_RANK, HEADS * LANES)
    wv = wkv[:, :, HEAD_DIM:].reshape(KV_RANK, fw)
    wukv = jnp.concatenate([wk, wv], axis=1).astype(BF16)
    return (norm_pre[None, :], wmain, bfrow, tri, emat, q_norm[None, :], wuq, kv_norm[None, :], wukv)


def kernel(x, meta, norm_pre, norm_post, w_in, b_f, q_norm, w_uq, kv_norm, w_ukv, w_out):
    b, s, d = x.shape
    depth = norm_pre.shape[0]
    assert depth == 1, "meta tokens are not carried between layers; only DEPTH == 1 is implemented"
    assert s % PROJ_ROWS == 0 and s % ATT_T == 0 and PROJ_ROWS % ATT_T == 0 and ATT_T % CHUNK == 0

    frame_tabs = _rope_tables(jnp.arange(N_META, N_META + s, dtype=F32))
    meta_tabs = _rope_tables(jnp.arange(META_TILE, dtype=F32))
    xm = jnp.zeros((1, META_TILE, d), x.dtype).at[0, :N_META].set(meta.astype(x.dtype))

    h = x
    for l in range(depth):
        consts = _layer_consts(norm_pre[l], w_in[l], b_f[l], q_norm[l], w_uq[l], kv_norm[l], w_ukv[l])
        mconsts = consts[:3] + (consts[3][:META_TILE, :META_TILE],) + consts[4:]
        q, k, vt, gates = _run_proj(h, consts, frame_tabs, n_real=None, kv_tile=ATT_T, rows=PROJ_ROWS)
        _, km, vmt, _ = _run_proj(xm, mconsts, meta_tabs, n_real=N_META, kv_tile=META_TILE, rows=META_TILE)
        vmt = vmt.reshape(2 * HEADS * HEAD_DIM // LANES, LANES, META_TILE)
        mix = _run_attn(q, k, vt, km[0], vmt, gates, t=ATT_T)
        h = _run_out(h.reshape(b * s, d), mix.reshape(b * s, -1), w_out[l].astype(BF16),
                     norm_post[l][None, :], rows=PROJ_ROWS).reshape(b, s, d)
    return h
```

```python
import functools

import jax
import jax.numpy as jnp
import numpy as np
from jax import lax
from jax.experimental import pallas as pl
from jax.experimental.pallas import tpu as pltpu

F32 = jnp.float32
BF16 = jnp.bfloat16

LANES = 128
EPS = 1e-6
NEG_INF = -1e30
ROPE_BASE = 10000.0
CHUNK = 64
CHUNK_SHIFT = 6
N_META = 16
META_TILE = 128

HEADS = 8
HEAD_DIM = 64
ROPE_DIM = 32
HALF_ROPE = ROPE_DIM // 2
Q_RANK = 256
KV_RANK = 128
FOX_SCALE = HEAD_DIM ** -0.5
MLA_SCALE = (HEAD_DIM + ROPE_DIM) ** -0.5

FOX_BIAS_LANE = HEAD_DIM
MLA_PAD_LANE = HEAD_DIM + ROPE_DIM

PROJ_ROWS = 512
ATT_T = 512
VMEM_LIMIT = 56 * 1024 * 1024


def _rms(x, g):
    return x * lax.rsqrt(jnp.mean(x * x, axis=-1, keepdims=True) + EPS) * g


def _split3(x):
    hi = x.astype(BF16).astype(F32)
    r = x - hi
    mid = r.astype(BF16).astype(F32)
    lo = (r - mid).astype(BF16).astype(F32)
    return hi, mid, lo


def _rope(v, c, s1, s2):
    return v * c + pltpu.roll(v, LANES - HALF_ROPE, 1) * s1 + pltpu.roll(v, HALF_ROPE, 1) * s2


def _proj_kernel(x_ref, gpre_ref, wmain_ref, bf_ref, tri_ref, emat_ref, qn_ref, wuq_ref,
                 kvn_ref, wukv_ref, rc_ref, rs1_ref, rs2_ref,
                 q_ref, k_ref, vt_ref, gate_ref, carry_ref, *, n_real, kv_tile):
    tl = x_ref.shape[1]
    fw = HEADS * HEAD_DIM

    @pl.when(pl.program_id(1) == 0)
    def _():
        carry_ref[...] = jnp.zeros_like(carry_ref)

    x = x_ref[0]
    h = _rms(x, gpre_ref[...]).astype(BF16)
    proj = jnp.dot(h, wmain_ref[...], preferred_element_type=F32)
    o_fq, o_fk, o_fv, o_gate, o_cq, o_ckv, o_small = 0, fw, 2 * fw, 3 * fw, 5 * fw, 5 * fw + Q_RANK, 5 * fw + Q_RANK + KV_RANK

    lane = lax.broadcasted_iota(jnp.int32, (tl, LANES), 1)
    rowi = lax.broadcasted_iota(jnp.int32, (tl, LANES), 0)

    small = proj[:, o_small:o_small + LANES]
    z = small + bf_ref[...]
    lf = jnp.minimum(z, 0.0) - jnp.log1p(jnp.exp(-jnp.abs(z)))
    hi, mid, lo = _split3(lf)
    lf_split = jnp.where(lane < 8, hi, jnp.where(lane < 16, mid, jnp.where(lane < 24, lo, 0.0)))
    c3 = jnp.dot(tri_ref[...], lf_split.astype(BF16), preferred_element_type=F32)
    cum = c3 + pltpu.roll(c3, LANES - 8, 1) + pltpu.roll(c3, LANES - 16, 1)
    cum = cum + carry_ref[...]
    carry_ref[...] = cum[tl - 1:tl, :]
    if n_real is None:
        nck = -cum
    else:
        nck = jnp.where(rowi < n_real, cum[n_real - 1:n_real, :] - cum, NEG_INF)
    m8 = lane < 8
    nh, nm, nl = _split3(nck)
    ck_split = (jnp.where(m8, nh, 0.0) + pltpu.roll(jnp.where(m8, nm, 0.0), 8, 1)
                + pltpu.roll(jnp.where(m8, nl, 0.0), 16, 1))
    extras = jnp.dot(ck_split.astype(BF16), emat_ref[...], preferred_element_type=F32)

    q_const = jnp.where((lane >= FOX_BIAS_LANE) & (lane < FOX_BIAS_LANE + 3), 1.0, 0.0)
    head_lane = lane < HEAD_DIM
    for hp in range(HEADS // 2):
        qv = proj[:, o_fq + hp * LANES:o_fq + (hp + 1) * LANES] * FOX_SCALE
        kv = proj[:, o_fk + hp * LANES:o_fk + (hp + 1) * LANES]
        for sub in range(2):
            g = 2 * hp + sub
            qs = qv if sub == 0 else pltpu.roll(qv, HEAD_DIM, 1)
            ks = kv if sub == 0 else pltpu.roll(kv, HEAD_DIM, 1)
            q_ref[0, :, g * LANES:(g + 1) * LANES] = jnp.where(head_lane, qs, q_const).astype(BF16)
            k_ref[0, :, g * LANES:(g + 1) * LANES] = jnp.where(
                head_lane, ks, extras[:, g * LANES:(g + 1) * LANES]).astype(BF16)

    rc, rs1, rs2 = rc_ref[...], rs1_ref[...], rs2_ref[...]
    cqn = _rms(proj[:, o_cq:o_cq + Q_RANK], qn_ref[...]).astype(BF16)
    qa = jnp.dot(cqn, wuq_ref[...], preferred_element_type=F32) * MLA_SCALE
    q_one = jnp.where(lane == MLA_PAD_LANE, 1.0, 0.0)
    for hh in range(HEADS):
        g = HEADS + hh
        qh = qa[:, hh * LANES:(hh + 1) * LANES]
        q_ref[0, :, g * LANES:(g + 1) * LANES] = (_rope(qh, rc, rs1, rs2) + q_one).astype(BF16)

    ckvn = _rms(proj[:, o_ckv:o_ckv + KV_RANK], kvn_ref[...]).astype(BF16)
    kvu = jnp.dot(ckvn, wukv_ref[...], preferred_element_type=F32)
    kr = _rope(small, rc, rs1, rs2)
    kr = jnp.where((lane >= HEAD_DIM) & (lane < HEAD_DIM + ROPE_DIM), kr, 0.0)
    if n_real is not None:
        kr = jnp.where((rowi >= n_real) & (lane == MLA_PAD_LANE), NEG_INF, kr)
    for hh in range(HEADS):
        g = HEADS + hh
        k_ref[0, :, g * LANES:(g + 1) * LANES] = (kvu[:, hh * LANES:(hh + 1) * LANES] + kr).astype(BF16)

    vcat = jnp.concatenate([proj[:, o_fv:o_fv + fw], kvu[:, HEADS * LANES:HEADS * LANES + fw]], axis=1)
    for s in range(tl // kv_tile):
        vt_ref[0, s] = vcat[s * kv_tile:(s + 1) * kv_tile, :].T.astype(BF16)

    gt = proj[:, o_gate:o_gate + 2 * fw]
    gate_ref[0] = (gt / (1.0 + jnp.exp(-gt))).astype(BF16)


def _run_proj(x3, consts, tabs, *, n_real, kv_tile, rows):
    b, s, d = x3.shape
    nt = s // rows
    gpre, wmain, bfrow, tri, emat, qn, wuq, kvn, wukv = consts
    rc, rs1, rs2 = tabs
    nh = 2 * HEADS
    const = lambda a: pl.BlockSpec(a.shape, lambda bi, i: (0,) * a.ndim)
    tab = pl.BlockSpec((rows, LANES), lambda bi, i: (i, 0))
    return pl.pallas_call(
        functools.partial(_proj_kernel, n_real=n_real, kv_tile=kv_tile),
        grid=(b, nt),
        in_specs=[pl.BlockSpec((1, rows, d), lambda bi, i: (bi, i, 0)),
                  const(gpre), const(wmain), const(bfrow), const(tri), const(emat), const(qn),
                  const(wuq), const(kvn), const(wukv), tab, tab, tab],
        out_specs=[pl.BlockSpec((1, rows, nh * LANES), lambda bi, i: (bi, i, 0)),
                   pl.BlockSpec((1, rows, nh * LANES), lambda bi, i: (bi, i, 0)),
                   pl.BlockSpec((1, rows // kv_tile, nh * HEAD_DIM, kv_tile), lambda bi, i: (bi, i, 0, 0)),
                   pl.BlockSpec((1, rows, nh * HEAD_DIM), lambda bi, i: (bi, i, 0))],
        out_shape=[jax.ShapeDtypeStruct((b, s, nh * LANES), BF16),
                   jax.ShapeDtypeStruct((b, s, nh * LANES), BF16),
                   jax.ShapeDtypeStruct((b, s // kv_tile, nh * HEAD_DIM, kv_tile), BF16),
                   jax.ShapeDtypeStruct((b, s, nh * HEAD_DIM), BF16)],
        scratch_shapes=[pltpu.VMEM((1, LANES), F32)],
        compiler_params=pltpu.CompilerParams(
            dimension_semantics=("arbitrary", "arbitrary"), vmem_limit_bytes=VMEM_LIMIT),
        name="proj",
    )(x3, gpre, wmain, bfrow, tri, emat, qn, wuq, kvn, wukv, rc, rs1, rs2)


def _dot_nt(a, b):
    return lax.dot_general(a, b, (((1,), (1,)), ((), ())), preferred_element_type=F32)


def _attn_kernel(q_ref, k_ref, vt_ref, km_ref, vmt_ref, gate_ref, o_ref, *, t):
    s_len = q_ref.shape[1]
    pair = pl.program_id(1)
    shift = jnp.where(pair >= HEADS // 2, CHUNK_SHIFT, 0)
    row = lax.broadcasted_iota(jnp.int32, (t, t), 0)
    col = lax.broadcasted_iota(jnp.int32, (t, t), 1)
    diag_ok = jnp.right_shift(row, shift) <= jnp.right_shift(col, shift)
    out_row = lax.broadcasted_iota(jnp.int32, (LANES, t), 0)

    def tile_update(state, qs, k0, vt, masked):
        new = []
        for hd in range(2):
            m, l, acc = state[3 * hd:3 * hd + 3]
            kt = k_ref[0, pl.ds(k0, t), hd * LANES:(hd + 1) * LANES]
            s = _dot_nt(kt, qs[hd])
            if masked:
                s = jnp.where(diag_ok, s, NEG_INF)
            m_new = jnp.maximum(m, jnp.max(s, axis=0, keepdims=True))
            alpha = jnp.exp(m - m_new)
            p = jnp.exp(s - m_new)
            l = alpha * l + jnp.sum(p, axis=0, keepdims=True)
            acc = alpha * acc + jnp.dot(vt, p.astype(BF16), preferred_element_type=F32)
            new += [m_new, l, acc]
        return tuple(new)

    def q_block(i, carry):
        q0 = pl.multiple_of(i * t, t)
        qs = [q_ref[0, pl.ds(q0, t), hd * LANES:(hd + 1) * LANES] for hd in range(2)]
        state = []
        for hd in range(2):
            s = _dot_nt(km_ref[:, hd * LANES:(hd + 1) * LANES], qs[hd])
            m = jnp.max(s, axis=0, keepdims=True)
            p = jnp.exp(s - m)
            l = jnp.sum(p, axis=0, keepdims=True)
            acc = jnp.dot(vmt_ref[0], p.astype(BF16), preferred_element_type=F32)
            state += [m, l, acc]
        state = tuple(state)

        def full_tile(j, st):
            return tile_update(st, qs, pl.multiple_of(j * t, t), vt_ref[0, j], False)

        state = lax.fori_loop(0, i, full_tile, state)
        state = tile_update(state, qs, q0, vt_ref[0, i], True)
        o0 = state[2] / state[1]
        o1 = state[5] / state[4]
        o = jnp.where(out_row < HEAD_DIM, o0, o1)
        gate = gate_ref[0, pl.ds(q0, t), :].astype(F32)
        o_ref[0, pl.ds(q0, t), :] = (o.T * gate).astype(BF16)
        return carry

    lax.fori_loop(0, s_len // t, q_block, 0)


def _run_attn(q, k, vt, km, vmt, gates, *, t):
    b, s, _ = q.shape
    npairs = HEADS
    return pl.pallas_call(
        functools.partial(_attn_kernel, t=t),
        grid=(b, npairs),
        in_specs=[pl.BlockSpec((1, s, 2 * LANES), lambda bi, p: (bi, 0, p)),
                  pl.BlockSpec((1, s, 2 * LANES), lambda bi, p: (bi, 0, p)),
                  pl.BlockSpec((1, s // t, LANES, t), lambda bi, p: (bi, 0, p, 0)),
                  pl.BlockSpec((META_TILE, 2 * LANES), lambda bi, p: (0, p)),
                  pl.BlockSpec((1, LANES, META_TILE), lambda bi, p: (p, 0, 0)),
                  pl.BlockSpec((1, s, LANES), lambda bi, p: (bi, 0, p))],
        out_specs=pl.BlockSpec((1, s, LANES), lambda bi, p: (bi, 0, p)),
        out_shape=jax.ShapeDtypeStruct((b, s, npairs * LANES), BF16),
        compiler_params=pltpu.CompilerParams(
            dimension_semantics=("arbitrary", "arbitrary"), vmem_limit_bytes=VMEM_LIMIT),
        name="attn",
    )(q, k, vt, km, vmt, gates)


def _out_kernel(x_ref, mix_ref, w_ref, g_ref, o_ref):
    y = jnp.dot(mix_ref[...], w_ref[...], preferred_element_type=F32)
    o_ref[...] = x_ref[...] + _rms(y, g_ref[...])


def _run_out(x2, mix2, w, g, *, rows):
    n, d = x2.shape
    return pl.pallas_call(
        _out_kernel,
        grid=(n // rows,),
        in_specs=[pl.BlockSpec((rows, d), lambda i: (i, 0)),
                  pl.BlockSpec((rows, mix2.shape[1]), lambda i: (i, 0)),
                  pl.BlockSpec(w.shape, lambda i: (0, 0)),
                  pl.BlockSpec(g.shape, lambda i: (0, 0))],
        out_specs=pl.BlockSpec((rows, d), lambda i: (i, 0)),
        out_shape=jax.ShapeDtypeStruct((n, d), F32),
        compiler_params=pltpu.CompilerParams(
            dimension_semantics=("arbitrary",), vmem_limit_bytes=VMEM_LIMIT),
        name="outproj",
    )(x2, mix2, w, g)


def _rope_tables(pos):
    inv_freq = ROPE_BASE ** (-jnp.arange(0, ROPE_DIM, 2, dtype=F32) / ROPE_DIM)
    ang = pos[:, None] * inv_freq[None, :]
    cos, sin = jnp.cos(ang), jnp.sin(ang)
    n = pos.shape[0]
    z = lambda w: jnp.zeros((n, w), F32)
    rc = jnp.concatenate([jnp.ones((n, HEAD_DIM), F32), cos, cos, z(LANES - HEAD_DIM - ROPE_DIM)], axis=1)
    rs1 = jnp.concatenate([z(HEAD_DIM), -sin, z(LANES - HEAD_DIM - HALF_ROPE)], axis=1)
    rs2 = jnp.concatenate([z(HEAD_DIM + HALF_ROPE), sin, z(LANES - HEAD_DIM - ROPE_DIM)], axis=1)
    return rc, rs1, rs2


def _layer_consts(norm_pre, w_in, b_f, q_norm, w_uq, kv_norm, w_ukv):
    d = w_in.shape[0]
    fw = HEADS * HEAD_DIM
    splits = np.cumsum([fw, fw, fw, HEADS, fw, Q_RANK, KV_RANK, ROPE_DIM, fw])
    w_fq, w_fk, w_fv, w_fl, w_fg, w_cq, w_ckv, w_kr, w_mg = jnp.split(w_in, splits[:-1].tolist(), axis=1)
    zc = lambda w: jnp.zeros((d, w), w_in.dtype)
    w_small = jnp.concatenate([w_fl, w_fl, w_fl, zc(HEAD_DIM - 3 * HEADS), w_kr,
                               zc(LANES - HEAD_DIM - ROPE_DIM)], axis=1)
    wmain = jnp.concatenate([w_fq, w_fk, w_fv, w_fg, w_mg, w_cq, w_ckv, w_small], axis=1).astype(BF16)
    bfrow = jnp.concatenate([b_f, b_f, b_f, jnp.zeros((LANES - 3 * HEADS,), F32)])[None, :]
    tri = jnp.tril(jnp.ones((PROJ_ROWS, PROJ_ROWS), BF16))
    e = np.zeros((LANES, HEADS * LANES), np.float32)
    for piece in range(3):
        for hh in range(HEADS):
            e[piece * HEADS + hh, hh * LANES + FOX_BIAS_LANE + piece] = 1.0
    emat = jnp.asarray(e, BF16)
    qk = HEAD_DIM + ROPE_DIM
    wuq = jnp.pad(w_uq.reshape(Q_RANK, HEADS, qk), ((0, 0), (0, 0), (0, LANES - qk)))
    wuq = wuq.reshape(Q_RANK, HEADS * LANES).astype(BF16)
    wkv = w_ukv.reshape(KV_RANK, HEADS, 2 * HEAD_DIM)
    wk = jnp.pad(wkv[:, :, :HEAD_DIM], ((0, 0), (0, 0), (0, LANES - HEAD_DIM))).reshape(KV_RANK, HEADS * LANES)
    wv = wkv[:, :, HEAD_DIM:].reshape(KV_RANK, fw)
    wukv = jnp.concatenate([wk, wv], axis=1).astype(BF16)
    return (norm_pre[None, :], wmain, bfrow, tri, emat, q_norm[None, :], wuq, kv_norm[None, :], wukv)


def kernel(x, meta, norm_pre, norm_post, w_in, b_f, q_norm, w_uq, kv_norm, w_ukv, w_out):
    b, s, d = x.shape
    depth = norm_pre.shape[0]
    assert depth == 1, "meta tokens are not carried between layers; only DEPTH == 1 is implemented"
    assert s % PROJ_ROWS == 0 and s % ATT_T == 0 and PROJ_ROWS % ATT_T == 0 and ATT_T % CHUNK == 0

    frame_tabs = _rope_tables(jnp.arange(N_META, N_META + s, dtype=F32))
    meta_tabs = _rope_tables(jnp.arange(META_TILE, dtype=F32))
    xm = jnp.zeros((1, META_TILE, d), x.dtype).at[0, :N_META].set(meta.astype(x.dtype))

    h = x
    for l in range(depth):
        consts = _layer_consts(norm_pre[l], w_in[l], b_f[l], q_norm[l], w_uq[l], kv_norm[l], w_ukv[l])
        mconsts = consts[:3] + (consts[3][:META_TILE, :META_TILE],) + consts[4:]
        q, k, vt, gates = _run_proj(h, consts, frame_tabs, n_real=None, kv_tile=ATT_T, rows=PROJ_ROWS)
        _, km, vmt, _ = _run_proj(xm, mconsts, meta_tabs, n_real=N_META, kv_tile=META_TILE, rows=META_TILE)
        vmt = vmt.reshape(2 * HEADS * HEAD_DIM // LANES, LANES, META_TILE)
        mix = _run_attn(q, k, vt, km[0], vmt, gates, t=ATT_T)
        h = _run_out(h.reshape(b * s, d), mix.reshape(b * s, -1), w_out[l].astype(BF16),
                     norm_post[l][None, :], rows=PROJ_ROWS).reshape(b, s, d)
    return h
```

```python
import functools

import jax
import jax.numpy as jnp
import numpy as np
from jax import lax
from jax.experimental import pallas as pl
from jax.experimental.pallas import tpu as pltpu

F32 = jnp.float32
BF16 = jnp.bfloat16

LANES = 128
EPS = 1e-6
NEG_INF = -1e30
ROPE_BASE = 10000.0
CHUNK = 64
CHUNK_SHIFT = 6
N_META = 16
META_TILE = 128

HEADS = 8
HEAD_DIM = 64
ROPE_DIM = 32
HALF_ROPE = ROPE_DIM // 2
Q_RANK = 256
KV_RANK = 128
LOG2E = 1.4426950408889634
FOX_SCALE = HEAD_DIM ** -0.5 * LOG2E
MLA_SCALE = (HEAD_DIM + ROPE_DIM) ** -0.5 * LOG2E
ONES_ROWS = 16

FOX_BIAS_LANE = HEAD_DIM
MLA_PAD_LANE = HEAD_DIM + ROPE_DIM

PROJ_ROWS = 512
ATT_T = 512
VMEM_LIMIT = 56 * 1024 * 1024


def _rms(x, g):
    return x * lax.rsqrt(jnp.mean(x * x, axis=-1, keepdims=True) + EPS) * g


def _split3(x):
    hi = x.astype(BF16).astype(F32)
    r = x - hi
    mid = r.astype(BF16).astype(F32)
    lo = (r - mid).astype(BF16).astype(F32)
    return hi, mid, lo


def _rope(v, c, s1, s2):
    return v * c + pltpu.roll(v, LANES - HALF_ROPE, 1) * s1 + pltpu.roll(v, HALF_ROPE, 1) * s2


def _proj_kernel(x_ref, gpre_ref, wmain_ref, bf_ref, tri_ref, emat_ref, qn_ref, wuq_ref,
                 kvn_ref, wukv_ref, rc_ref, rs1_ref, rs2_ref,
                 q_ref, k_ref, vt_ref, gate_ref, carry_ref, *, n_real, kv_tile):
    tl = x_ref.shape[1]
    fw = HEADS * HEAD_DIM

    @pl.when(pl.program_id(1) == 0)
    def _():
        carry_ref[...] = jnp.zeros_like(carry_ref)

    x = x_ref[0]
    h = _rms(x, gpre_ref[...]).astype(BF16)
    proj = jnp.dot(h, wmain_ref[...], preferred_element_type=F32)
    o_fq, o_fk, o_fv, o_gate, o_cq, o_ckv, o_small = 0, fw, 2 * fw, 3 * fw, 5 * fw, 5 * fw + Q_RANK, 5 * fw + Q_RANK + KV_RANK

    lane = lax.broadcasted_iota(jnp.int32, (tl, LANES), 1)
    rowi = lax.broadcasted_iota(jnp.int32, (tl, LANES), 0)

    small = proj[:, o_small:o_small + LANES]
    z = small + bf_ref[...]
    lf = jnp.minimum(z, 0.0) - jnp.log1p(jnp.exp(-jnp.abs(z)))
    hi, mid, lo = _split3(lf)
    lf_split = jnp.where(lane < 8, hi, jnp.where(lane < 16, mid, jnp.where(lane < 24, lo, 0.0)))
    c3 = jnp.dot(tri_ref[...], lf_split.astype(BF16), preferred_element_type=F32)
    cum = c3 + pltpu.roll(c3, LANES - 8, 1) + pltpu.roll(c3, LANES - 16, 1)
    cum = cum + carry_ref[...]
    carry_ref[...] = cum[tl - 1:tl, :]
    if n_real is None:
        nck = -LOG2E * cum
    else:
        nck = jnp.where(rowi < n_real, LOG2E * (cum[n_real - 1:n_real, :] - cum), NEG_INF)
    m8 = lane < 8
    nh, nm, nl = _split3(nck)
    ck_split = (jnp.where(m8, nh, 0.0) + pltpu.roll(jnp.where(m8, nm, 0.0), 8, 1)
                + pltpu.roll(jnp.where(m8, nl, 0.0), 16, 1))
    extras = jnp.dot(ck_split.astype(BF16), emat_ref[...], preferred_element_type=F32)

    q_const = jnp.where((lane >= FOX_BIAS_LANE) & (lane < FOX_BIAS_LANE + 3), 1.0, 0.0)
    head_lane = lane < HEAD_DIM
    for hp in range(HEADS // 2):
        qv = proj[:, o_fq + hp * LANES:o_fq + (hp + 1) * LANES] * FOX_SCALE
        kv = proj[:, o_fk + hp * LANES:o_fk + (hp + 1) * LANES]
        for sub in range(2):
            g = 2 * hp + sub
            qs = qv if sub == 0 else pltpu.roll(qv, HEAD_DIM, 1)
            ks = kv if sub == 0 else pltpu.roll(kv, HEAD_DIM, 1)
            q_ref[0, :, g * LANES:(g + 1) * LANES] = jnp.where(head_lane, qs, q_const).astype(BF16)
            k_ref[0, :, g * LANES:(g + 1) * LANES] = jnp.where(
                head_lane, ks, extras[:, g * LANES:(g + 1) * LANES]).astype(BF16)

    rc, rs1, rs2 = rc_ref[...], rs1_ref[...], rs2_ref[...]
    cqn = _rms(proj[:, o_cq:o_cq + Q_RANK], qn_ref[...]).astype(BF16)
    qa = jnp.dot(cqn, wuq_ref[...], preferred_element_type=F32) * MLA_SCALE
    q_one = jnp.where(lane == MLA_PAD_LANE, 1.0, 0.0)
    for hh in range(HEADS):
        g = HEADS + hh
        qh = qa[:, hh * LANES:(hh + 1) * LANES]
        q_ref[0, :, g * LANES:(g + 1) * LANES] = (_rope(qh, rc, rs1, rs2) + q_one).astype(BF16)

    ckvn = _rms(proj[:, o_ckv:o_ckv + KV_RANK], kvn_ref[...]).astype(BF16)
    kvu = jnp.dot(ckvn, wukv_ref[...], preferred_element_type=F32)
    kr = _rope(small, rc, rs1, rs2)
    kr = jnp.where((lane >= HEAD_DIM) & (lane < HEAD_DIM + ROPE_DIM), kr, 0.0)
    if n_real is not None:
        kr = jnp.where((rowi >= n_real) & (lane == MLA_PAD_LANE), NEG_INF, kr)
    for hh in range(HEADS):
        g = HEADS + hh
        k_ref[0, :, g * LANES:(g + 1) * LANES] = (kvu[:, hh * LANES:(hh + 1) * LANES] + kr).astype(BF16)

    vcat = jnp.concatenate([proj[:, o_fv:o_fv + fw], kvu[:, HEADS * LANES:HEADS * LANES + fw]], axis=1)
    for s in range(tl // kv_tile):
        vt_ref[0, s] = vcat[s * kv_tile:(s + 1) * kv_tile, :].T.astype(BF16)

    gt = proj[:, o_gate:o_gate + 2 * fw]
    gate_ref[0] = (gt / (1.0 + jnp.exp(-gt))).astype(BF16)


def _run_proj(x3, consts, tabs, *, n_real, kv_tile, rows):
    b, s, d = x3.shape
    nt = s // rows
    gpre, wmain, bfrow, tri, emat, qn, wuq, kvn, wukv = consts
    rc, rs1, rs2 = tabs
    nh = 2 * HEADS
    const = lambda a: pl.BlockSpec(a.shape, lambda bi, i: (0,) * a.ndim)
    tab = pl.BlockSpec((rows, LANES), lambda bi, i: (i, 0))
    return pl.pallas_call(
        functools.partial(_proj_kernel, n_real=n_real, kv_tile=kv_tile),
        grid=(b, nt),
        in_specs=[pl.BlockSpec((1, rows, d), lambda bi, i: (bi, i, 0)),
                  const(gpre), const(wmain), const(bfrow), const(tri), const(emat), const(qn),
                  const(wuq), const(kvn), const(wukv), tab, tab, tab],
        out_specs=[pl.BlockSpec((1, rows, nh * LANES), lambda bi, i: (bi, i, 0)),
                   pl.BlockSpec((1, rows, nh * LANES), lambda bi, i: (bi, i, 0)),
                   pl.BlockSpec((1, rows // kv_tile, nh * HEAD_DIM, kv_tile), lambda bi, i: (bi, i, 0, 0)),
                   pl.BlockSpec((1, rows, nh * HEAD_DIM), lambda bi, i: (bi, i, 0))],
        out_shape=[jax.ShapeDtypeStruct((b, s, nh * LANES), BF16),
                   jax.ShapeDtypeStruct((b, s, nh * LANES), BF16),
                   jax.ShapeDtypeStruct((b, s // kv_tile, nh * HEAD_DIM, kv_tile), BF16),
                   jax.ShapeDtypeStruct((b, s, nh * HEAD_DIM), BF16)],
        scratch_shapes=[pltpu.VMEM((1, LANES), F32)],
        compiler_params=pltpu.CompilerParams(
            dimension_semantics=("arbitrary", "arbitrary"), vmem_limit_bytes=VMEM_LIMIT),
        name="proj",
    )(x3, gpre, wmain, bfrow, tri, emat, qn, wuq, kvn, wukv, rc, rs1, rs2)


def _dot_nt(a, b):
    return lax.dot_general(a, b, (((1,), (1,)), ((), ())), preferred_element_type=F32)


def _attn_kernel(q_ref, k_ref, vt_ref, km_ref, vmt_ref, gate_ref, o_ref, sa_ref, sb_ref, *, t):
    s_len = q_ref.shape[1]
    pair = pl.program_id(1)
    shift = jnp.where(pair >= HEADS // 2, CHUNK_SHIFT, 0)
    row = lax.broadcasted_iota(jnp.int32, (t, t), 0)
    col = lax.broadcasted_iota(jnp.int32, (t, t), 1)
    diag_ok = jnp.right_shift(row, shift) <= jnp.right_shift(col, shift)

    def ones_block(n):
        r = lax.broadcasted_iota(jnp.int32, (ONES_ROWS, n), 0)
        return jnp.where(r == 0, 1.0, 0.0).astype(BF16)

    ones_t, ones_m = ones_block(t), ones_block(META_TILE)

    def produce(buf, qs, j):
        k0 = pl.multiple_of(j * t, t)
        for hd in range(2):
            buf[hd] = _dot_nt(k_ref[0, pl.ds(k0, t), hd * LANES:(hd + 1) * LANES], qs[hd])

    def consume(buf, state, j, masked):
        new = []
        for hd in range(2):
            m, acc = state[2 * hd:2 * hd + 2]
            s = buf[hd]
            if masked:
                s = jnp.where(diag_ok, s, NEG_INF)
            m_new = jnp.maximum(m, jnp.max(s, axis=0, keepdims=True))
            alpha = jnp.exp2(m - m_new)
            p = jnp.exp2(s - m_new).astype(BF16)
            vt = jnp.concatenate([vt_ref[0, j, hd * HEAD_DIM:(hd + 1) * HEAD_DIM, :], ones_t], axis=0)
            acc = alpha * acc + jnp.dot(vt, p, preferred_element_type=F32)
            new += [m_new, acc]
        return tuple(new)

    def q_block(i, odd):
        q0 = pl.multiple_of(i * t, t)
        qs = [q_ref[0, pl.ds(q0, t), hd * LANES:(hd + 1) * LANES] for hd in range(2)]
        produce(sa_ref, qs, 0)
        state = []
        for hd in range(2):
            s = _dot_nt(km_ref[:, hd * LANES:(hd + 1) * LANES], qs[hd])
            m = jnp.max(s, axis=0, keepdims=True)
            p = jnp.exp2(s - m).astype(BF16)
            vmt = jnp.concatenate([vmt_ref[0, hd * HEAD_DIM:(hd + 1) * HEAD_DIM, :], ones_m], axis=0)
            state += [m, jnp.dot(vmt, p, preferred_element_type=F32)]
        state = tuple(state)

        def tile_pair(pj, st):
            j = 2 * pj
            produce(sb_ref, qs, j + 1)
            st = consume(sa_ref, st, j, False)
            produce(sa_ref, qs, j + 2)
            return consume(sb_ref, st, j + 1, False)

        state = lax.fori_loop(0, i // 2, tile_pair, state)
        if odd:
            produce(sb_ref, qs, i)
            state = consume(sa_ref, state, i - 1, False)
            state = consume(sb_ref, state, i, True)
        else:
            state = consume(sa_ref, state, i, True)
        o = jnp.concatenate([state[2 * hd + 1][:HEAD_DIM] / state[2 * hd + 1][HEAD_DIM:HEAD_DIM + 1]
                             for hd in range(2)], axis=0)
        gate = gate_ref[0, pl.ds(q0, t), :].astype(F32)
        o_ref[0, pl.ds(q0, t), :] = (o.T * gate).astype(BF16)

    def q_block_pair(u, carry):
        q_block(2 * u, False)
        q_block(2 * u + 1, True)
        return carry

    lax.fori_loop(0, s_len // (2 * t), q_block_pair, 0)


def _run_attn(q, k, vt, km, vmt, gates, *, t):
    b, s, _ = q.shape
    npairs = HEADS
    return pl.pallas_call(
        functools.partial(_attn_kernel, t=t),
        grid=(b, npairs),
        in_specs=[pl.BlockSpec((1, s, 2 * LANES), lambda bi, p: (bi, 0, p)),
                  pl.BlockSpec((1, s, 2 * LANES), lambda bi, p: (bi, 0, p)),
                  pl.BlockSpec((1, s // t, LANES, t), lambda bi, p: (bi, 0, p, 0)),
                  pl.BlockSpec((META_TILE, 2 * LANES), lambda bi, p: (0, p)),
                  pl.BlockSpec((1, LANES, META_TILE), lambda bi, p: (p, 0, 0)),
                  pl.BlockSpec((1, s, LANES), lambda bi, p: (bi, 0, p))],
        out_specs=pl.BlockSpec((1, s, LANES), lambda bi, p: (bi, 0, p)),
        out_shape=jax.ShapeDtypeStruct((b, s, npairs * LANES), BF16),
        scratch_shapes=[pltpu.VMEM((2, t, t), F32), pltpu.VMEM((2, t, t), F32)],
        compiler_params=pltpu.CompilerParams(
            dimension_semantics=("arbitrary", "arbitrary"), vmem_limit_bytes=VMEM_LIMIT),
        name="attn",
    )(q, k, vt, km, vmt, gates)


def _out_kernel(x_ref, mix_ref, w_ref, g_ref, o_ref):
    y = jnp.dot(mix_ref[...], w_ref[...], preferred_element_type=F32)
    o_ref[...] = x_ref[...] + _rms(y, g_ref[...])


def _run_out(x2, mix2, w, g, *, rows):
    n, d = x2.shape
    return pl.pallas_call(
        _out_kernel,
        grid=(n // rows,),
        in_specs=[pl.BlockSpec((rows, d), lambda i: (i, 0)),
                  pl.BlockSpec((rows, mix2.shape[1]), lambda i: (i, 0)),
                  pl.BlockSpec(w.shape, lambda i: (0, 0)),
                  pl.BlockSpec(g.shape, lambda i: (0, 0))],
        out_specs=pl.BlockSpec((rows, d), lambda i: (i, 0)),
        out_shape=jax.ShapeDtypeStruct((n, d), F32),
        compiler_params=pltpu.CompilerParams(
            dimension_semantics=("arbitrary",), vmem_limit_bytes=VMEM_LIMIT),
        name="outproj",
    )(x2, mix2, w, g)


def _rope_tables(pos):
    inv_freq = ROPE_BASE ** (-jnp.arange(0, ROPE_DIM, 2, dtype=F32) / ROPE_DIM)
    ang = pos[:, None] * inv_freq[None, :]
    cos, sin = jnp.cos(ang), jnp.sin(ang)
    n = pos.shape[0]
    z = lambda w: jnp.zeros((n, w), F32)
    rc = jnp.concatenate([jnp.ones((n, HEAD_DIM), F32), cos, cos, z(LANES - HEAD_DIM - ROPE_DIM)], axis=1)
    rs1 = jnp.concatenate([z(HEAD_DIM), -sin, z(LANES - HEAD_DIM - HALF_ROPE)], axis=1)
    rs2 = jnp.concatenate([z(HEAD_DIM + HALF_ROPE), sin, z(LANES - HEAD_DIM - ROPE_DIM)], axis=1)
    return rc, rs1, rs2


def _layer_consts(norm_pre, w_in, b_f, q_norm, w_uq, kv_norm, w_ukv):
    d = w_in.shape[0]
    fw = HEADS * HEAD_DIM
    splits = np.cumsum([fw, fw, fw, HEADS, fw, Q_RANK, KV_RANK, ROPE_DIM, fw])
    w_fq, w_fk, w_fv, w_fl, w_fg, w_cq, w_ckv, w_kr, w_mg = jnp.split(w_in, splits[:-1].tolist(), axis=1)
    zc = lambda w: jnp.zeros((d, w), w_in.dtype)
    w_small = jnp.concatenate([w_fl, w_fl, w_fl, zc(HEAD_DIM - 3 * HEADS), w_kr,
                               zc(LANES - HEAD_DIM - ROPE_DIM)], axis=1)
    wmain = jnp.concatenate([w_fq, w_fk, w_fv, w_fg, w_mg, w_cq, w_ckv, w_small], axis=1).astype(BF16)
    bfrow = jnp.concatenate([b_f, b_f, b_f, jnp.zeros((LANES - 3 * HEADS,), F32)])[None, :]
    tri = jnp.tril(jnp.ones((PROJ_ROWS, PROJ_ROWS), BF16))
    e = np.zeros((LANES, HEADS * LANES), np.float32)
    for piece in range(3):
        for hh in range(HEADS):
            e[piece * HEADS + hh, hh * LANES + FOX_BIAS_LANE + piece] = 1.0
    emat = jnp.asarray(e, BF16)
    qk = HEAD_DIM + ROPE_DIM
    wuq = jnp.pad(w_uq.reshape(Q_RANK, HEADS, qk), ((0, 0), (0, 0), (0, LANES - qk)))
    wuq = wuq.reshape(Q_RANK, HEADS * LANES).astype(BF16)
    wkv = w_ukv.reshape(KV_RANK, HEADS, 2 * HEAD_DIM)
    wk = jnp.pad(wkv[:, :, :HEAD_DIM], ((0, 0), (0, 0), (0, LANES - HEAD_DIM))).reshape(KV_RANK, HEADS * LANES)
    wv = wkv[:, :, HEAD_DIM:].reshape(KV_RANK, fw)
    wukv = jnp.concatenate([wk, wv], axis=1).astype(BF16)
    return (norm_pre[None, :], wmain, bfrow, tri, emat, q_norm[None, :], wuq, kv_norm[None, :], wukv)


def kernel(x, meta, norm_pre, norm_post, w_in, b_f, q_norm, w_uq, kv_norm, w_ukv, w_out):
    b, s, d = x.shape
    depth = norm_pre.shape[0]
    assert depth == 1, "meta tokens are not carried between layers; only DEPTH == 1 is implemented"
    assert s % PROJ_ROWS == 0 and s % (2 * ATT_T) == 0 and PROJ_ROWS % ATT_T == 0 and ATT_T % CHUNK == 0

    frame_tabs = _rope_tables(jnp.arange(N_META, N_META + s, dtype=F32))
    meta_tabs = _rope_tables(jnp.arange(META_TILE, dtype=F32))
    xm = jnp.zeros((1, META_TILE, d), x.dtype).at[0, :N_META].set(meta.astype(x.dtype))

    h = x
    for l in range(depth):
        consts = _layer_consts(norm_pre[l], w_in[l], b_f[l], q_norm[l], w_uq[l], kv_norm[l], w_ukv[l])
        mconsts = consts[:3] + (consts[3][:META_TILE, :META_TILE],) + consts[4:]
        q, k, vt, gates = _run_proj(h, consts, frame_tabs, n_real=None, kv_tile=ATT_T, rows=PROJ_ROWS)
        _, km, vmt, _ = _run_proj(xm, mconsts, meta_tabs, n_real=N_META, kv_tile=META_TILE, rows=META_TILE)
        vmt = vmt.reshape(2 * HEADS * HEAD_DIM // LANES, LANES, META_TILE)
        mix = _run_attn(q, k, vt, km[0], vmt, gates, t=ATT_T)
        h = _run_out(h.reshape(b * s, d), mix.reshape(b * s, -1), w_out[l].astype(BF16),
                     norm_post[l][None, :], rows=PROJ_ROWS).reshape(b, s, d)
    return h
```

```python
import functools

import jax
import jax.numpy as jnp
import numpy as np
from jax import lax
from jax.experimental import pallas as pl
from jax.experimental.pallas import tpu as pltpu

F32 = jnp.float32
BF16 = jnp.bfloat16

LANES = 128
EPS = 1e-6
NEG_INF = -1e30
ROPE_BASE = 10000.0
CHUNK = 64
CHUNK_SHIFT = 6
N_META = 16
META_TILE = 128

HEADS = 8
HEAD_DIM = 64
ROPE_DIM = 32
HALF_ROPE = ROPE_DIM // 2
Q_RANK = 256
KV_RANK = 128
LOG2E = 1.4426950408889634
FOX_SCALE = HEAD_DIM ** -0.5 * LOG2E
MLA_SCALE = (HEAD_DIM + ROPE_DIM) ** -0.5 * LOG2E
ONES_ROWS = 16

FOX_BIAS_LANE = HEAD_DIM
MLA_PAD_LANE = HEAD_DIM + ROPE_DIM

PROJ_ROWS = 512
ATT_T = 512
VMEM_LIMIT = 56 * 1024 * 1024


def _rms(x, g):
    return x * lax.rsqrt(jnp.mean(x * x, axis=-1, keepdims=True) + EPS) * g


def _split3(x):
    hi = x.astype(BF16).astype(F32)
    r = x - hi
    mid = r.astype(BF16).astype(F32)
    lo = (r - mid).astype(BF16).astype(F32)
    return hi, mid, lo


def _rope(v, c, s1, s2):
    return v * c + pltpu.roll(v, LANES - HALF_ROPE, 1) * s1 + pltpu.roll(v, HALF_ROPE, 1) * s2


def _proj_kernel(x_ref, gpre_ref, wmain_ref, bf_ref, tri_ref, emat_ref, qn_ref, wuq_ref,
                 kvn_ref, wukv_ref, rc_ref, rs1_ref, rs2_ref,
                 q_ref, k_ref, vt_ref, gate_ref, carry_ref, *, n_real, kv_tile):
    tl = x_ref.shape[1]
    fw = HEADS * HEAD_DIM

    @pl.when(pl.program_id(1) == 0)
    def _():
        carry_ref[...] = jnp.zeros_like(carry_ref)

    x = x_ref[0]
    h = _rms(x, gpre_ref[...]).astype(BF16)
    proj = jnp.dot(h, wmain_ref[...], preferred_element_type=F32)
    o_fq, o_fk, o_fv, o_gate, o_cq, o_ckv, o_small = 0, fw, 2 * fw, 3 * fw, 5 * fw, 5 * fw + Q_RANK, 5 * fw + Q_RANK + KV_RANK

    lane = lax.broadcasted_iota(jnp.int32, (tl, LANES), 1)
    rowi = lax.broadcasted_iota(jnp.int32, (tl, LANES), 0)

    small = proj[:, o_small:o_small + LANES]
    z = small + bf_ref[...]
    lf = jnp.minimum(z, 0.0) - jnp.log1p(jnp.exp(-jnp.abs(z)))
    hi, mid, lo = _split3(lf)
    lf_split = jnp.where(lane < 8, hi, jnp.where(lane < 16, mid, jnp.where(lane < 24, lo, 0.0)))
    c3 = jnp.dot(tri_ref[...], lf_split.astype(BF16), preferred_element_type=F32)
    cum = c3 + pltpu.roll(c3, LANES - 8, 1) + pltpu.roll(c3, LANES - 16, 1)
    cum = cum + carry_ref[...]
    carry_ref[...] = cum[tl - 1:tl, :]
    if n_real is None:
        nck = -LOG2E * cum
    else:
        nck = jnp.where(rowi < n_real, LOG2E * (cum[n_real - 1:n_real, :] - cum), NEG_INF)
    m8 = lane < 8
    nh, nm, nl = _split3(nck)
    ck_split = (jnp.where(m8, nh, 0.0) + pltpu.roll(jnp.where(m8, nm, 0.0), 8, 1)
                + pltpu.roll(jnp.where(m8, nl, 0.0), 16, 1))
    extras = jnp.dot(ck_split.astype(BF16), emat_ref[...], preferred_element_type=F32)

    q_const = jnp.where((lane >= FOX_BIAS_LANE) & (lane < FOX_BIAS_LANE + 3), 1.0, 0.0)
    head_lane = lane < HEAD_DIM
    for hp in range(HEADS // 2):
        qv = proj[:, o_fq + hp * LANES:o_fq + (hp + 1) * LANES] * FOX_SCALE
        kv = proj[:, o_fk + hp * LANES:o_fk + (hp + 1) * LANES]
        for sub in range(2):
            g = 2 * hp + sub
            qs = qv if sub == 0 else pltpu.roll(qv, HEAD_DIM, 1)
            ks = kv if sub == 0 else pltpu.roll(kv, HEAD_DIM, 1)
            q_ref[0, :, g * LANES:(g + 1) * LANES] = jnp.where(head_lane, qs, q_const).astype(BF16)
            k_ref[0, :, g * LANES:(g + 1) * LANES] = jnp.where(
                head_lane, ks, extras[:, g * LANES:(g + 1) * LANES]).astype(BF16)

    rc, rs1, rs2 = rc_ref[...], rs1_ref[...], rs2_ref[...]
    cqn = _rms(proj[:, o_cq:o_cq + Q_RANK], qn_ref[...]).astype(BF16)
    qa = jnp.dot(cqn, wuq_ref[...], preferred_element_type=F32) * MLA_SCALE
    q_one = jnp.where(lane == MLA_PAD_LANE, 1.0, 0.0)
    for hh in range(HEADS):
        g = HEADS + hh
        qh = qa[:, hh * LANES:(hh + 1) * LANES]
        q_ref[0, :, g * LANES:(g + 1) * LANES] = (_rope(qh, rc, rs1, rs2) + q_one).astype(BF16)

    ckvn = _rms(proj[:, o_ckv:o_ckv + KV_RANK], kvn_ref[...]).astype(BF16)
    kvu = jnp.dot(ckvn, wukv_ref[...], preferred_element_type=F32)
    kr = _rope(small, rc, rs1, rs2)
    kr = jnp.where((lane >= HEAD_DIM) & (lane < HEAD_DIM + ROPE_DIM), kr, 0.0)
    if n_real is not None:
        kr = jnp.where((rowi >= n_real) & (lane == MLA_PAD_LANE), NEG_INF, kr)
    for hh in range(HEADS):
        g = HEADS + hh
        k_ref[0, :, g * LANES:(g + 1) * LANES] = (kvu[:, hh * LANES:(hh + 1) * LANES] + kr).astype(BF16)

    vcat = jnp.concatenate([proj[:, o_fv:o_fv + fw], kvu[:, HEADS * LANES:HEADS * LANES + fw]], axis=1)
    for s in range(tl // kv_tile):
        vt_ref[0, s] = vcat[s * kv_tile:(s + 1) * kv_tile, :].T.astype(BF16)

    gt = proj[:, o_gate:o_gate + 2 * fw]
    gate_ref[0] = (gt / (1.0 + jnp.exp(-gt))).astype(BF16)


def _run_proj(x3, consts, tabs, *, n_real, kv_tile, rows):
    b, s, d = x3.shape
    nt = s // rows
    gpre, wmain, bfrow, tri, emat, qn, wuq, kvn, wukv = consts
    rc, rs1, rs2 = tabs
    nh = 2 * HEADS
    const = lambda a: pl.BlockSpec(a.shape, lambda bi, i: (0,) * a.ndim)
    tab = pl.BlockSpec((rows, LANES), lambda bi, i: (i, 0))
    return pl.pallas_call(
        functools.partial(_proj_kernel, n_real=n_real, kv_tile=kv_tile),
        grid=(b, nt),
        in_specs=[pl.BlockSpec((1, rows, d), lambda bi, i: (bi, i, 0)),
                  const(gpre), const(wmain), const(bfrow), const(tri), const(emat), const(qn),
                  const(wuq), const(kvn), const(wukv), tab, tab, tab],
        out_specs=[pl.BlockSpec((1, rows, nh * LANES), lambda bi, i: (bi, i, 0)),
                   pl.BlockSpec((1, rows, nh * LANES), lambda bi, i: (bi, i, 0)),
                   pl.BlockSpec((1, rows // kv_tile, nh * HEAD_DIM, kv_tile), lambda bi, i: (bi, i, 0, 0)),
                   pl.BlockSpec((1, rows, nh * HEAD_DIM), lambda bi, i: (bi, i, 0))],
        out_shape=[jax.ShapeDtypeStruct((b, s, nh * LANES), BF16),
                   jax.ShapeDtypeStruct((b, s, nh * LANES), BF16),
                   jax.ShapeDtypeStruct((b, s // kv_tile, nh * HEAD_DIM, kv_tile), BF16),
                   jax.ShapeDtypeStruct((b, s, nh * HEAD_DIM), BF16)],
        scratch_shapes=[pltpu.VMEM((1, LANES), F32)],
        compiler_params=pltpu.CompilerParams(
            dimension_semantics=("arbitrary", "arbitrary"), vmem_limit_bytes=VMEM_LIMIT),
        name="proj",
    )(x3, gpre, wmain, bfrow, tri, emat, qn, wuq, kvn, wukv, rc, rs1, rs2)


def _dot_nt(a, b):
    return lax.dot_general(a, b, (((1,), (1,)), ((), ())), preferred_element_type=F32)


def _attn_kernel(q_ref, k_ref, vt_ref, km_ref, vmt_ref, gate_ref, o_ref,
                 sa_ref, sb_ref, c0_ref, c1_ref, *, t):
    s_len = q_ref.shape[1]
    c_refs = (c0_ref, c1_ref)
    pair = pl.program_id(1)
    shift = jnp.where(pair >= HEADS // 2, CHUNK_SHIFT, 0)
    row = lax.broadcasted_iota(jnp.int32, (t + META_TILE, t), 0)
    col = lax.broadcasted_iota(jnp.int32, (t + META_TILE, t), 1)
    first_ok = (jnp.right_shift(row, shift) <= jnp.right_shift(col, shift)) | (row >= t)

    def ones_block(n):
        r = lax.broadcasted_iota(jnp.int32, (ONES_ROWS, n), 0)
        return jnp.where(r == 0, 1.0, 0.0).astype(BF16)

    ones_t, ones_first = ones_block(t), ones_block(t + META_TILE)

    def produce(buf, qs, j):
        k0 = pl.multiple_of(j * t, t)
        for hd in range(2):
            buf[hd] = _dot_nt(k_ref[0, pl.ds(k0, t), hd * LANES:(hd + 1) * LANES], qs[hd])

    def load_q(i):
        q0 = pl.multiple_of(i * t, t)
        return [q_ref[0, pl.ds(q0, t), hd * LANES:(hd + 1) * LANES] for hd in range(2)]

    def produce_first(cbuf, qs, i):
        k0 = pl.multiple_of(i * t, t)
        for hd in range(2):
            hl = slice(hd * LANES, (hd + 1) * LANES)
            cbuf[hd, 0:t] = _dot_nt(k_ref[0, pl.ds(k0, t), hl], qs[hd])
            cbuf[hd, t:t + META_TILE] = _dot_nt(km_ref[:, hl], qs[hd])

    def consume_first(cbuf, i):
        state = []
        for hd in range(2):
            hr = slice(hd * HEAD_DIM, (hd + 1) * HEAD_DIM)
            s = jnp.where(first_ok, cbuf[hd], NEG_INF)
            m = jnp.max(s, axis=0, keepdims=True)
            p = jnp.exp2(s - m).astype(BF16)
            vt = jnp.concatenate([vt_ref[0, i, hr, :], vmt_ref[0, hr, :]], axis=1)
            vt = jnp.concatenate([vt, ones_first], axis=0)
            state += [m, jnp.dot(vt, p, preferred_element_type=F32)]
        return tuple(state)

    def consume(buf, state, j):
        new = []
        for hd in range(2):
            m, acc = state[2 * hd:2 * hd + 2]
            s = buf[hd]
            m_new = jnp.maximum(m, jnp.max(s, axis=0, keepdims=True))
            alpha = jnp.exp2(m - m_new)
            p = jnp.exp2(s - m_new).astype(BF16)
            vt = jnp.concatenate([vt_ref[0, j, hd * HEAD_DIM:(hd + 1) * HEAD_DIM, :], ones_t], axis=0)
            acc = alpha * acc + jnp.dot(vt, p, preferred_element_type=F32)
            new += [m_new, acc]
        return tuple(new)

    def q_block(i, parity, last=False):
        qs = load_q(i)
        cur_c, nxt_c = c_refs[parity], c_refs[1 - parity]

        def produce_next():
            if not last:
                produce_first(nxt_c, load_q(i + 1), i + 1)

        def tile_pair(pj, st):
            j = 2 * pj
            produce(sb_ref, qs, j + 1)
            st = consume(sa_ref, st, j)
            produce(sa_ref, qs, j + 2)
            return consume(sb_ref, st, j + 1)

        if isinstance(i, int) and i == 0:
            produce_next()
            state = consume_first(cur_c, i)
        elif parity == 1:
            produce(sa_ref, qs, 0)
            state = consume_first(cur_c, i)
            state = lax.fori_loop(0, (i - 1) // 2, tile_pair, state)
            produce_next()
            state = consume(sa_ref, state, i - 1)
        else:
            produce(sa_ref, qs, 0)
            state = consume_first(cur_c, i)
            state = lax.fori_loop(0, i // 2 - 1, tile_pair, state)
            produce(sb_ref, qs, i - 1)
            state = consume(sa_ref, state, i - 2)
            produce_next()
            state = consume(sb_ref, state, i - 1)
        o = jnp.concatenate([state[2 * hd + 1][:HEAD_DIM] / state[2 * hd + 1][HEAD_DIM:HEAD_DIM + 1]
                             for hd in range(2)], axis=0)
        q0 = pl.multiple_of(i * t, t)
        gate = gate_ref[0, pl.ds(q0, t), :].astype(F32)
        o_ref[0, pl.ds(q0, t), :] = (o.T * gate).astype(BF16)

    def q_block_pair(u, carry):
        q_block(2 * u - 1, 1)
        q_block(2 * u, 0)
        return carry

    nq = s_len // t
    produce_first(c_refs[0], load_q(0), 0)
    q_block(0, 0)
    lax.fori_loop(1, nq // 2, q_block_pair, 0)
    q_block(nq - 1, 1, last=True)


def _run_attn(q, k, vt, km, vmt, gates, *, t):
    b, s, _ = q.shape
    npairs = HEADS
    return pl.pallas_call(
        functools.partial(_attn_kernel, t=t),
        grid=(b, npairs),
        in_specs=[pl.BlockSpec((1, s, 2 * LANES), lambda bi, p: (bi, 0, p)),
                  pl.BlockSpec((1, s, 2 * LANES), lambda bi, p: (bi, 0, p)),
                  pl.BlockSpec((1, s // t, LANES, t), lambda bi, p: (bi, 0, p, 0)),
                  pl.BlockSpec((META_TILE, 2 * LANES), lambda bi, p: (0, p)),
                  pl.BlockSpec((1, LANES, META_TILE), lambda bi, p: (p, 0, 0)),
                  pl.BlockSpec((1, s, LANES), lambda bi, p: (bi, 0, p))],
        out_specs=pl.BlockSpec((1, s, LANES), lambda bi, p: (bi, 0, p)),
        out_shape=jax.ShapeDtypeStruct((b, s, npairs * LANES), BF16),
        scratch_shapes=[pltpu.VMEM((2, t, t), F32), pltpu.VMEM((2, t, t), F32),
                        pltpu.VMEM((2, t + META_TILE, t), F32), pltpu.VMEM((2, t + META_TILE, t), F32)],
        compiler_params=pltpu.CompilerParams(
            dimension_semantics=("arbitrary", "arbitrary"), vmem_limit_bytes=VMEM_LIMIT),
        name="attn",
    )(q, k, vt, km, vmt, gates)


def _out_kernel(x_ref, mix_ref, w_ref, g_ref, o_ref):
    y = jnp.dot(mix_ref[...], w_ref[...], preferred_element_type=F32)
    o_ref[...] = x_ref[...] + _rms(y, g_ref[...])


def _run_out(x2, mix2, w, g, *, rows):
    n, d = x2.shape
    return pl.pallas_call(
        _out_kernel,
        grid=(n // rows,),
        in_specs=[pl.BlockSpec((rows, d), lambda i: (i, 0)),
                  pl.BlockSpec((rows, mix2.shape[1]), lambda i: (i, 0)),
                  pl.BlockSpec(w.shape, lambda i: (0, 0)),
                  pl.BlockSpec(g.shape, lambda i: (0, 0))],
        out_specs=pl.BlockSpec((rows, d), lambda i: (i, 0)),
        out_shape=jax.ShapeDtypeStruct((n, d), F32),
        compiler_params=pltpu.CompilerParams(
            dimension_semantics=("arbitrary",), vmem_limit_bytes=VMEM_LIMIT),
        name="outproj",
    )(x2, mix2, w, g)


def _rope_tables(pos):
    inv_freq = ROPE_BASE ** (-jnp.arange(0, ROPE_DIM, 2, dtype=F32) / ROPE_DIM)
    ang = pos[:, None] * inv_freq[None, :]
    cos, sin = jnp.cos(ang), jnp.sin(ang)
    n = pos.shape[0]
    z = lambda w: jnp.zeros((n, w), F32)
    rc = jnp.concatenate([jnp.ones((n, HEAD_DIM), F32), cos, cos, z(LANES - HEAD_DIM - ROPE_DIM)], axis=1)
    rs1 = jnp.concatenate([z(HEAD_DIM), -sin, z(LANES - HEAD_DIM - HALF_ROPE)], axis=1)
    rs2 = jnp.concatenate([z(HEAD_DIM + HALF_ROPE), sin, z(LANES - HEAD_DIM - ROPE_DIM)], axis=1)
    return rc, rs1, rs2


def _layer_consts(norm_pre, w_in, b_f, q_norm, w_uq, kv_norm, w_ukv):
    d = w_in.shape[0]
    fw = HEADS * HEAD_DIM
    splits = np.cumsum([fw, fw, fw, HEADS, fw, Q_RANK, KV_RANK, ROPE_DIM, fw])
    w_fq, w_fk, w_fv, w_fl, w_fg, w_cq, w_ckv, w_kr, w_mg = jnp.split(w_in, splits[:-1].tolist(), axis=1)
    zc = lambda w: jnp.zeros((d, w), w_in.dtype)
    w_small = jnp.concatenate([w_fl, w_fl, w_fl, zc(HEAD_DIM - 3 * HEADS), w_kr,
                               zc(LANES - HEAD_DIM - ROPE_DIM)], axis=1)
    wmain = jnp.concatenate([w_fq, w_fk, w_fv, w_fg, w_mg, w_cq, w_ckv, w_small], axis=1).astype(BF16)
    bfrow = jnp.concatenate([b_f, b_f, b_f, jnp.zeros((LANES - 3 * HEADS,), F32)])[None, :]
    tri = jnp.tril(jnp.ones((PROJ_ROWS, PROJ_ROWS), BF16))
    e = np.zeros((LANES, HEADS * LANES), np.float32)
    for piece in range(3):
        for hh in range(HEADS):
            e[piece * HEADS + hh, hh * LANES + FOX_BIAS_LANE + piece] = 1.0
    emat = jnp.asarray(e, BF16)
    qk = HEAD_DIM + ROPE_DIM
    wuq = jnp.pad(w_uq.reshape(Q_RANK, HEADS, qk), ((0, 0), (0, 0), (0, LANES - qk)))
    wuq = wuq.reshape(Q_RANK, HEADS * LANES).astype(BF16)
    wkv = w_ukv.reshape(KV_RANK, HEADS, 2 * HEAD_DIM)
    wk = jnp.pad(wkv[:, :, :HEAD_DIM], ((0, 0), (0, 0), (0, LANES - HEAD_DIM))).reshape(KV_RANK, HEADS * LANES)
    wv = wkv[:, :, HEAD_DIM:].reshape(KV_RANK, fw)
    wukv = jnp.concatenate([wk, wv], axis=1).astype(BF16)
    return (norm_pre[None, :], wmain, bfrow, tri, emat, q_norm[None, :], wuq, kv_norm[None, :], wukv)


def kernel(x, meta, norm_pre, norm_post, w_in, b_f, q_norm, w_uq, kv_norm, w_ukv, w_out):
    b, s, d = x.shape
    depth = norm_pre.shape[0]
    assert depth == 1, "meta tokens are not carried between layers; only DEPTH == 1 is implemented"
    assert s % PROJ_ROWS == 0 and s % (2 * ATT_T) == 0 and PROJ_ROWS % ATT_T == 0 and ATT_T % CHUNK == 0

    frame_tabs = _rope_tables(jnp.arange(N_META, N_META + s, dtype=F32))
    meta_tabs = _rope_tables(jnp.arange(META_TILE, dtype=F32))
    xm = jnp.zeros((1, META_TILE, d), x.dtype).at[0, :N_META].set(meta.astype(x.dtype))

    h = x
    for l in range(depth):
        consts = _layer_consts(norm_pre[l], w_in[l], b_f[l], q_norm[l], w_uq[l], kv_norm[l], w_ukv[l])
        mconsts = consts[:3] + (consts[3][:META_TILE, :META_TILE],) + consts[4:]
        q, k, vt, gates = _run_proj(h, consts, frame_tabs, n_real=None, kv_tile=ATT_T, rows=PROJ_ROWS)
        _, km, vmt, _ = _run_proj(xm, mconsts, meta_tabs, n_real=N_META, kv_tile=META_TILE, rows=META_TILE)
        vmt = vmt.reshape(2 * HEADS * HEAD_DIM // LANES, LANES, META_TILE)
        mix = _run_attn(q, k, vt, km[0], vmt, gates, t=ATT_T)
        h = _run_out(h.reshape(b * s, d), mix.reshape(b * s, -1), w_out[l].astype(BF16),
                     norm_post[l][None, :], rows=PROJ_ROWS).reshape(b, s, d)
    return h
```
